```python
import math
import jax, jax.numpy as jnp
from jax import lax
import numpy as np

D_MODEL = 1024
BATCH = 16
SEQ = 256
DEPTH = 4
DEC_BATCH = 2
DEC_SEQ = 4096
PAST_LEN = 512

GRID_W = 64
N_MIXERS = 3
N_CONV = (DEPTH + 2) // 3
N_FNET = (DEPTH + 1) // 3
N_SSM = DEPTH // 3
N_DENSE = (DEPTH + 1) // 2
N_MOE = DEPTH // 2
CONV_W = 3
FNET_GROUPS = 8
FNET_GC = D_MODEL // FNET_GROUPS
SSM_GC = 16
SSM_GROUPS = D_MODEL // SSM_GC
SSM_P = 64
D_FF = 4096
N_EXPERTS = 8
TOP_K = 2
MOE_D_FF = 2048
N_MOD = 6
EPS = 1e-6
DT_MIN = 1e-3
DT_MAX = 1e-1
POS_BASE = 10000.0

kernel_name = "hybrid_conv_fnet_s5_prefix_diffusion_step"

F32 = jnp.float32


def rmsnorm(x, g):
    xf = x.astype(F32)
    y = xf * lax.rsqrt(jnp.mean(xf * xf, axis=-1, keepdims=True) + EPS)
    return (y * g.astype(F32)).astype(x.dtype)


def modulation(cond, w, b):
    m = jnp.dot(jax.nn.silu(cond), w) + b
    return m.reshape(m.shape[:-1] + (N_MOD, D_MODEL))


def mod_k(m, k):
    return jnp.expand_dims(m[..., k, :], -2)


def grid_pos_embed(n_tok, dtype):
    rows = n_tok // GRID_W
    r, col = jnp.meshgrid(jnp.arange(rows, dtype=F32), jnp.arange(GRID_W, dtype=F32), indexing="ij")
    quarter = D_MODEL // 4
    omega = 1.0 / (POS_BASE ** (jnp.arange(quarter, dtype=F32) / quarter))
    ang_r = r.reshape(-1)[:, None] * omega
    ang_c = col.reshape(-1)[:, None] * omega
    pe = jnp.concatenate([jnp.sin(ang_r), jnp.cos(ang_r), jnp.sin(ang_c), jnp.cos(ang_c)], axis=-1)
    return pe.astype(dtype)


def short_conv_mixer(h, w_in, k, w_out):
    b_g, c_g, v = jnp.split(jnp.dot(h, w_in), 3, axis=-1)
    z = c_g * v
    L = z.shape[1]
    zp = jnp.pad(z, ((0, 0), (1, 1), (0, 0)))
    zc = k[0] * zp[:, :L] + k[1] * zp[:, 1:L + 1] + k[2] * zp[:, 2:]
    return jnp.dot(b_g * zc, w_out)


def fourier_mixer(h, w_out):
    b, L, _ = h.shape
    hg = h.astype(F32).reshape(b, L, FNET_GROUPS, FNET_GC)
    f = jnp.fft.fft2(hg, axes=(1, 3), norm="ortho").real
    return jnp.dot(f.reshape(b, L, D_MODEL).astype(h.dtype), w_out)


def _ssm_combine(e1, e2):
    a1, b1 = e1
    a2, b2 = e2
    return a1 * a2, a2 * b1 + b2


def ssm_direction(u_c, lam_re, lam_im, b_re, b_im, c_re, c_im, log_dt, h0, reverse, want_state):
    lam = lax.complex(lam_re.astype(F32), lam_im.astype(F32))
    dt = jnp.exp(log_dt.astype(F32))[:, None]
    lam_bar = jnp.exp(lam * dt)
    bmat = lax.complex(b_re.astype(F32), b_im.astype(F32))
    b_bar = ((lam_bar - 1.0) / lam)[..., None] * bmat
    bu = jnp.einsum("blgh,gph->blgp", u_c, b_bar)
    if h0 is not None:
        edge = -1 if reverse else 0
        bu = bu.at[:, edge].add(lam_bar * h0)
    a = jnp.broadcast_to(lam_bar, bu.shape)
    _, hs = lax.associative_scan(_ssm_combine, (a, bu), reverse=reverse, axis=1)
    cmat = lax.complex(c_re.astype(F32), c_im.astype(F32))
    y = jnp.einsum("blgp,ghp->blgh", hs, cmat).real
    final = (hs[:, 0] if reverse else hs[:, -1]) if want_state else None
    return y, final


def ssm_mixer(h, lam_re, lam_im, b_re, b_im, c_re, c_im, log_dt, d_skip, w_glu, h0):
    b, L, _ = h.shape
    hf = h.astype(F32)
    u_c = hf.reshape(b, L, SSM_GROUPS, SSM_GC).astype(jnp.complex64)
    want = h0 is None
    ys, finals = [], []
    for d in range(2):
        y, s = ssm_direction(u_c, lam_re[d], lam_im[d], b_re[d], b_im[d], c_re[d], c_im[d], log_dt[d],
                             None if h0 is None else h0[:, d], reverse=(d == 1), want_state=want)
        ys.append(y)
        finals.append(s)
    y = (ys[0] + ys[1]).reshape(b, L, D_MODEL) + d_skip.astype(F32) * hf
    g = jax.nn.gelu(y).astype(h.dtype)
    val, gate = jnp.split(jnp.dot(g, w_glu), 2, axis=-1)
    out = val * jax.nn.sigmoid(gate)
    if want:
        st = jnp.stack(finals, axis=1)
        return out, jnp.stack([st.real, st.imag], axis=-1)
    return out, None


def swiglu(h, wg, wu, wd):
    return jnp.dot(jax.nn.silu(jnp.dot(h, wg)) * jnp.dot(h, wu), wd)


def moe_ffn(h, router, wg, wu, wd):
    logits = jnp.dot(h, router).astype(F32)
    top_v, top_i = lax.top_k(logits, TOP_K)
    top_w = jax.nn.softmax(top_v, axis=-1)
    gates = jnp.sum(jax.nn.one_hot(top_i, N_EXPERTS, dtype=F32) * top_w[..., None], axis=-2)
    out = jnp.zeros(h.shape, F32)
    for e in range(N_EXPERTS):
        out = out + gates[..., e:e + 1] * swiglu(h, wg[e], wu[e], wd[e]).astype(F32)
    return out.astype(h.dtype)


def run_trunk(x, cond, state_ssm, ada_w, ada_b, norm_g, conv_w_in, conv_k, conv_w_out, fnet_w_out,
              ssm_lam_re, ssm_lam_im, ssm_b_re, ssm_b_im, ssm_c_re, ssm_c_im, ssm_log_dt, ssm_d, ssm_w_glu,
              ffn_w_gate, ffn_w_up, ffn_w_down, moe_router, moe_w_gate, moe_w_up, moe_w_down):
    states = []
    for l in range(DEPTH):
        m = modulation(cond, ada_w[l], ada_b[l])
        g = norm_g[l]
        h = rmsnorm(x, g[0]) * (1 + mod_k(m, 1)) + mod_k(m, 0)
        kind, j = l % N_MIXERS, l // N_MIXERS
        if kind == 0:
            y = short_conv_mixer(h, conv_w_in[j], conv_k[j], conv_w_out[j])
        elif kind == 1:
            y = fourier_mixer(h, fnet_w_out[j])
        else:
            h0 = None
            if state_ssm is not None:
                sj = state_ssm[:, j].astype(F32)
                h0 = lax.complex(sj[..., 0], sj[..., 1])
            y, s = ssm_mixer(h, ssm_lam_re[j], ssm_lam_im[j], ssm_b_re[j], ssm_b_im[j], ssm_c_re[j],
                             ssm_c_im[j], ssm_log_dt[j], ssm_d[j], ssm_w_glu[j], h0)
            if s is not None:
                states.append(s)
        x = x + mod_k(m, 2) * rmsnorm(y, g[1])
        h = rmsnorm(x, g[2]) * (1 + mod_k(m, 4)) + mod_k(m, 3)
        if l % 2 == 0:
            f = swiglu(h, ffn_w_gate[l // 2], ffn_w_up[l // 2], ffn_w_down[l // 2])
        else:
            f = moe_ffn(h, moe_router[l // 2], moe_w_gate[l // 2], moe_w_up[l // 2], moe_w_down[l // 2])
        x = x + mod_k(m, 5) * rmsnorm(f, g[3])
    return x, states


def setup_inputs(seed: int = 0) -> dict:
    key = jax.random.key(seed)
    ks = iter(jax.random.split(key, 40))
    D = D_MODEL

    def nrm(shape, scale):
        return jax.random.normal(next(ks), shape, F32) * scale

    n = jnp.arange(SSM_P, dtype=F32)
    return {
        "x_prompt": nrm((BATCH, SEQ, D), 1.0),
        "x_sample": nrm((DEC_BATCH, DEC_SEQ, D), 1.0),
        "state_ssm": nrm((DEC_BATCH, N_SSM, 2, SSM_GROUPS, SSM_P, 2), 0.1),
        "c": nrm((DEC_BATCH, D), 1.0),
        "c_ctx": nrm((D,), 1.0),
        "ada_w": nrm((DEPTH, D, N_MOD * D), 0.5 * D ** -0.5),
        "ada_b": nrm((DEPTH, N_MOD * D), 0.02),
        "norm_g": 1.0 + nrm((DEPTH, 4, D), 0.05),
        "conv_w_in": nrm((N_CONV, D, 3 * D), D ** -0.5),
        "conv_k": nrm((N_CONV, CONV_W, D), CONV_W ** -0.5),
        "conv_w_out": nrm((N_CONV, D, D), D ** -0.5),
        "fnet_w_out": nrm((N_FNET, D, D), D ** -0.5),
        "ssm_lam_re": -0.5 + nrm((N_SSM, 2, SSM_GROUPS, SSM_P), 0.01),
        "ssm_lam_im": math.pi * n + nrm((N_SSM, 2, SSM_GROUPS, SSM_P), 0.01),
        "ssm_b_re": nrm((N_SSM, 2, SSM_GROUPS, SSM_P, SSM_GC), (2 * SSM_GC) ** -0.5),
        "ssm_b_im": nrm((N_SSM, 2, SSM_GROUPS, SSM_P, SSM_GC), (2 * SSM_GC) ** -0.5),
        "ssm_c_re": nrm((N_SSM, 2, SSM_GROUPS, SSM_GC, SSM_P), (2 * SSM_P) ** -0.5),
        "ssm_c_im": nrm((N_SSM, 2, SSM_GROUPS, SSM_GC, SSM_P), (2 * SSM_P) ** -0.5),
        "ssm_log_dt": jax.random.uniform(next(ks), (N_SSM, 2, SSM_GROUPS), F32,
                                         math.log(DT_MIN), math.log(DT_MAX)),
        "ssm_d": nrm((N_SSM, D), 1.0),
        "ssm_w_glu": nrm((N_SSM, D, 2 * D), D ** -0.5),
        "ffn_w_gate": nrm((N_DENSE, D, D_FF), D ** -0.5),
        "ffn_w_up": nrm((N_DENSE, D, D_FF), D ** -0.5),
        "ffn_w_down": nrm((N_DENSE, D_FF, D), D_FF ** -0.5),
        "moe_router": nrm((N_MOE, D, N_EXPERTS), D ** -0.5),
        "moe_w_gate": nrm((N_MOE, N_EXPERTS, D, MOE_D_FF), D ** -0.5),
        "moe_w_up": nrm((N_MOE, N_EXPERTS, D, MOE_D_FF), D ** -0.5),
        "moe_w_down": nrm((N_MOE, N_EXPERTS, MOE_D_FF, D), MOE_D_FF ** -0.5),
    }


def reference(x_prompt, x_sample, state_ssm, c, c_ctx, ada_w, ada_b, norm_g, conv_w_in, conv_k,
              conv_w_out, fnet_w_out, ssm_lam_re, ssm_lam_im, ssm_b_re, ssm_b_im, ssm_c_re, ssm_c_im,
              ssm_log_dt, ssm_d, ssm_w_glu, ffn_w_gate, ffn_w_up, ffn_w_down, moe_router, moe_w_gate,
              moe_w_up, moe_w_down):
    weights = (ada_w, ada_b, norm_g, conv_w_in, conv_k, conv_w_out, fnet_w_out, ssm_lam_re, ssm_lam_im,
               ssm_b_re, ssm_b_im, ssm_c_re, ssm_c_im, ssm_log_dt, ssm_d, ssm_w_glu, ffn_w_gate, ffn_w_up,
               ffn_w_down, moe_router, moe_w_gate, moe_w_up, moe_w_down)
    y_prompt, ctx_states = run_trunk(x_prompt, c_ctx, None, *weights)
    new_state_ssm = jnp.stack(ctx_states, axis=1).astype(x_prompt.dtype)
    x_lat = x_sample + grid_pos_embed(x_sample.shape[1], x_sample.dtype)
    y_sample, _ = run_trunk(x_lat, c, state_ssm, *weights)
    return (y_prompt, y_sample, new_state_ssm)
```

```python
import functools
import math

import numpy as np
import jax
import jax.numpy as jnp
from jax import lax
from jax.experimental import pallas as pl
from jax.experimental.pallas import tpu as pltpu

F32 = jnp.float32
BF16 = jnp.bfloat16

D = 1024
T_CTX = 16 * 256
L_CTX = 256
L_LAT = 4096
N_LAT = 2
T = T_CTX + N_LAT * L_LAT
N_MOD = 6
EPS = 1e-6
N_EXPERTS = 8
LANES = 128
FNET_GC = 128
SSM_GC = 16
SSM_G = D // SSM_GC
SSM_P = 64
CHUNK = 16
N_CHUNK = T // CHUNK
GB = 8
VMEM_LIMIT = 56 * 1024 * 1024


def _cparams(sem):
    return pltpu.CompilerParams(dimension_semantics=sem, vmem_limit_bytes=VMEM_LIMIT)


def _rms(x, g):
    return x * lax.rsqrt(jnp.mean(x * x, axis=-1, keepdims=True) + EPS) * g


def _pre(x, g, scale, shift):
    return _rms(x, g) * (1.0 + scale) + shift


def _dot(a, b):
    return jnp.dot(a, b, preferred_element_type=F32)


def _mod_kernel(ct_ref, w_ref, b_ref, o_ref):
    c = ct_ref[...]
    s = c * jax.nn.sigmoid(c)
    w = w_ref[0]
    rows = [jnp.sum(w * s[:, r:r + 1], axis=0, keepdims=True) for r in range(3)]
    rows.append(jnp.zeros((5, w.shape[1]), F32))
    o_ref[0] = jnp.concatenate(rows, axis=0) + b_ref[0]


def _modulation(cond3, ada_w, ada_b):
    depth = ada_w.shape[0]
    n = ada_w.shape[2]
    tn = 512
    ct = jnp.zeros((D, LANES), F32).at[:, :3].set(cond3.T)
    out = pl.pallas_call(
        _mod_kernel,
        grid=(depth, n // tn),
        in_specs=[
            pl.BlockSpec((D, LANES), lambda l, j: (0, 0)),
            pl.BlockSpec((1, D, tn), lambda l, j: (l, 0, j)),
            pl.BlockSpec((1, 1, tn), lambda l, j: (l, 0, j)),
        ],
        out_specs=pl.BlockSpec((1, 8, tn), lambda l, j: (l, 0, j)),
        out_shape=jax.ShapeDtypeStruct((depth, 8, n), F32),
        compiler_params=_cparams(("parallel", "parallel")),
        name="modulation",
    )(ct, ada_w, ada_b.reshape(depth, 1, n))
    return out[:, :3].reshape(depth, 3, N_MOD, D)


def _conv_kernel(x_ref, xp_ref, xn_ref, m_ref, g_ref, k_ref, win_ref, wout_ref, o_ref, *, tm):
    i = pl.program_id(0)
    m = m_ref[0]
    g = g_ref[...]
    x = x_ref[...]

    def pre(xx):
        return _pre(xx, g[0:1], m[1:2], m[0:1]).astype(BF16)

    p = _dot(pre(x), win_ref[...])
    bg = p[:, :D]
    z = p[:, D:2 * D] * p[:, 2 * D:]
    pp = _dot(pre(xp_ref[...]), win_ref[:, D:])
    zp = (pp[:, :D] * pp[:, D:])[7:8]
    pn = _dot(pre(xn_ref[...]), win_ref[:, D:])
    zn = (pn[:, :D] * pn[:, D:])[0:1]

    loc = lax.broadcasted_iota(jnp.int32, (tm, 1), 0)
    row = loc + i * tm
    seq_len = jnp.where(row < T_CTX, L_CTX, L_LAT)
    pos = row & (seq_len - 1)
    zprev = jnp.where(loc == 0, zp, pltpu.roll(z, 1, 0))
    zprev = jnp.where(pos == 0, 0.0, zprev)
    znext = jnp.where(loc == tm - 1, zn, pltpu.roll(z, tm - 1, 0))
    znext = jnp.where(pos == seq_len - 1, 0.0, znext)
    k = k_ref[...]
    zc = k[0:1] * zprev + k[1:2] * z + k[2:3] * znext
    y = _dot((bg * zc).astype(BF16), wout_ref[...])
    o_ref[...] = x + m[2:3] * _rms(y, g[1:2])


def _conv_mixer(x, mod_l, g_l, conv_k, w_in, w_out):
    tm = 512
    nb = T // 8
    return pl.pallas_call(
        functools.partial(_conv_kernel, tm=tm),
        grid=(T // tm,),
        in_specs=[
            pl.BlockSpec((tm, D), lambda i: (i, 0)),
            pl.BlockSpec((8, D), lambda i: (jnp.maximum(i * (tm // 8) - 1, 0), 0)),
            pl.BlockSpec((8, D), lambda i: (jnp.minimum((i + 1) * (tm // 8), nb - 1), 0)),
            pl.BlockSpec((1, N_MOD, D), lambda i: ((i * tm) // T_CTX, 0, 0)),
            pl.BlockSpec((4, D), lambda i: (0, 0)),
            pl.BlockSpec((3, D), lambda i: (0, 0)),
            pl.BlockSpec((D, 3 * D), lambda i: (0, 0)),
            pl.BlockSpec((D, D), lambda i: (0, 0)),
        ],
        out_specs=pl.BlockSpec((tm, D), lambda i: (i, 0)),
        out_shape=jax.ShapeDtypeStruct((T, D), F32),
        compiler_params=_cparams(("parallel",)),
        name="conv_mixer",
    )(x, x, x, mod_l, g_l, conv_k, w_in.astype(BF16), w_out.astype(BF16))


def _ffn_kernel(x_ref, m_ref, g_ref, wg_ref, wu_ref, wd_ref, o_ref, h_scr, acc_scr):
    j = pl.program_id(1)
    m = m_ref[0]
    g = g_ref[...]

    @pl.when(j == 0)
    def _():
        h_scr[...] = _pre(x_ref[...], g[2:3], m[4:5], m[3:4]).astype(BF16)
        acc_scr[...] = jnp.zeros_like(acc_scr)

    h = h_scr[...]
    a = _dot(h, wg_ref[...])
    u = _dot(h, wu_ref[...])
    act = (a * jax.nn.sigmoid(a) * u).astype(BF16)
    acc_scr[...] += _dot(act, wd_ref[...])

    @pl.when(j == pl.num_programs(1) - 1)
    def _():
        o_ref[...] = x_ref[...] + m[5:6] * _rms(acc_scr[...], g[3:4])


def _ffn_dense(x, mod_l, g_l, wg, wu, wd):
    tm, tf = 1024, 512
    ff = wg.shape[1]
    return pl.pallas_call(
        _ffn_kernel,
        grid=(T // tm, ff // tf),
        in_specs=[
            pl.BlockSpec((tm, D), lambda i, j: (i, 0)),
            pl.BlockSpec((1, N_MOD, D), lambda i, j: ((i * tm) // T_CTX, 0, 0)),
            pl.BlockSpec((4, D), lambda i, j: (0, 0)),
            pl.BlockSpec((D, tf), lambda i, j: (0, j)),
            pl.BlockSpec((D, tf), lambda i, j: (0, j)),
            pl.BlockSpec((tf, D), lambda i, j: (j, 0)),
        ],
        out_specs=pl.BlockSpec((tm, D), lambda i, j: (i, 0)),
        out_shape=jax.ShapeDtypeStruct((T, D), F32),
        scratch_shapes=[pltpu.VMEM((tm, D), BF16), pltpu.VMEM((tm, D), F32)],
        compiler_params=_cparams(("parallel", "arbitrary")),
        name="ffn_dense",
    )(x, mod_l, g_l, wg.astype(BF16), wu.astype(BF16), wd.astype(BF16))


def _moe_kernel(x_ref, m_ref, g_ref, r_ref, wg_ref, wu_ref, wd_ref, o_ref, h_scr, gate_scr, acc_scr):
    e = pl.program_id(1)
    j = pl.program_id(2)
    m = m_ref[0]
    g = g_ref[...]
    tm = x_ref.shape[0]
    lane = lax.broadcasted_iota(jnp.int32, (tm, LANES), 1)

    @pl.when((e == 0) & (j == 0))
    def _():
        hf = _pre(x_ref[...], g[2:3], m[4:5], m[3:4])
        h_scr[...] = hf.astype(BF16)
        logits = jnp.dot(hf, r_ref[...], preferred_element_type=F32, precision=lax.Precision.HIGHEST)
        lanef = lane.astype(F32)
        lg = jnp.where(lane < N_EXPERTS, logits, -jnp.inf)
        v1 = jnp.max(lg, axis=-1, keepdims=True)
        i1 = jnp.min(jnp.where(lg == v1, lanef, float(LANES)), axis=-1, keepdims=True)
        lg2 = jnp.where(lanef == i1, -jnp.inf, lg)
        v2 = jnp.max(lg2, axis=-1, keepdims=True)
        i2 = jnp.min(jnp.where(lg2 == v2, lanef, float(LANES)), axis=-1, keepdims=True)
        e2 = jnp.exp(v2 - v1)
        w1 = 1.0 / (1.0 + e2)
        gate_scr[...] = jnp.where(lanef == i1, w1, 0.0) + jnp.where(lanef == i2, e2 * w1, 0.0)
        acc_scr[...] = jnp.zeros_like(acc_scr)

    h = h_scr[...]
    ge = jnp.sum(jnp.where(lane == e, gate_scr[...], 0.0), axis=-1, keepdims=True)
    a = _dot(h, wg_ref[0])
    u = _dot(h, wu_ref[0])
    act = (a * jax.nn.sigmoid(a) * u * ge).astype(BF16)
    acc_scr[...] += _dot(act, wd_ref[0])

    @pl.when((e == pl.num_programs(1) - 1) & (j == pl.num_programs(2) - 1))
    def _():
        o_ref[...] = x_ref[...] + m[5:6] * _rms(acc_scr[...], g[3:4])


def _ffn_moe(x, mod_l, g_l, router, wg, wu, wd):
    tm, tf = 1024, 512
    ne, _, ff = wg.shape
    rpad = jnp.zeros((D, LANES), F32).at[:, :ne].set(router)
    return pl.pallas_call(
        _moe_kernel,
        grid=(T // tm, ne, ff // tf),
        in_specs=[
            pl.BlockSpec((tm, D), lambda i, e, j: (i, 0)),
            pl.BlockSpec((1, N_MOD, D), lambda i, e, j: ((i * tm) // T_CTX, 0, 0)),
            pl.BlockSpec((4, D), lambda i, e, j: (0, 0)),
            pl.BlockSpec((D, LANES), lambda i, e, j: (0, 0)),
            pl.BlockSpec((1, D, tf), lambda i, e, j: (e, 0, j)),
            pl.BlockSpec((1, D, tf), lambda i, e, j: (e, 0, j)),
            pl.BlockSpec((1, tf, D), lambda i, e, j: (e, j, 0)),
        ],
        out_specs=pl.BlockSpec((tm, D), lambda i, e, j: (i, 0)),
        out_shape=jax.ShapeDtypeStruct((T, D), F32),
        scratch_shapes=[pltpu.VMEM((tm, D), BF16), pltpu.VMEM((tm, LANES), F32), pltpu.VMEM((tm, D), F32)],
        compiler_params=_cparams(("parallel", "arbitrary", "arbitrary")),
        name="ffn_moe",
    )(x, mod_l, g_l, rpad, wg.astype(BF16), wu.astype(BF16), wd.astype(BF16))


def _fnet_chan_kernel(x_ref, m_ref, g_ref, cs_ref, xc_ref, xs_ref):
    m = m_ref[0]
    g = g_ref[...]
    h = _pre(x_ref[...], g[0:1], m[1:2], m[0:1]).astype(BF16)
    cs = cs_ref[...]
    for j in range(D // FNET_GC):
        sl = slice(j * FNET_GC, (j + 1) * FNET_GC)
        z = _dot(h[:, sl], cs)
        xc_ref[:, sl] = z[:, :FNET_GC].astype(BF16)
        xs_ref[:, sl] = z[:, FNET_GC:].astype(BF16)


def _fnet_seq_kernel(x_ref, m_ref, g_ref, cl_ref, sl_ref, xc_ref, xs_ref, wout_ref, o_ref, acc_scr):
    kk = pl.program_id(2)

    @pl.when(kk == 0)
    def _():
        acc_scr[...] = jnp.zeros_like(acc_scr)

    acc_scr[...] += _dot(cl_ref[...], xc_ref[...]) - _dot(sl_ref[...], xs_ref[...])

    @pl.when(kk == pl.num_programs(2) - 1)
    def _():
        m = m_ref[0]
        g = g_ref[...]
        y = _dot(acc_scr[...].astype(BF16), wout_ref[...])
        o_ref[...] = x_ref[...] + m[2:3] * _rms(y, g[1:2])


def _dft_tables(n, scale):
    k = np.arange(n, dtype=np.int64)
    ph = (k[:, None] * k[None, :]) % n
    ang = 2.0 * np.pi * ph.astype(np.float64) / n
    return np.cos(ang) * scale, np.sin(ang) * scale


def _seq_dft_matrices(n):
    if n <= 256:
        c, s = _dft_tables(n, 1.0 / math.sqrt(n))
        return jnp.asarray(c, BF16), jnp.asarray(s, BF16)
    m = np.arange(n, dtype=np.int64)
    lo = np.arange(64, dtype=np.int64)
    a_lo = 2.0 * np.pi * ((lo[:, None] * m[None, :]) % n).astype(np.float64) / n
    a_hi = 2.0 * np.pi * ((64 * lo[: n // 64, None] * m[None, :]) % n).astype(np.float64) / n
    sc = 1.0 / math.sqrt(n)
    c1, s1 = jnp.asarray(np.cos(a_lo), F32), jnp.asarray(np.sin(a_lo), F32)
    c2, s2 = jnp.asarray(np.cos(a_hi) * sc, F32), jnp.asarray(np.sin(a_hi) * sc, F32)
    cm = c2[:, None, :] * c1[None, :, :] - s2[:, None, :] * s1[None, :, :]
    sm = s2[:, None, :] * c1[None, :, :] + c2[:, None, :] * s1[None, :, :]
    return cm.reshape(n, n).astype(BF16), sm.reshape(n, n).astype(BF16)


def _fnet_seq(x, xc, xs, mod_l, g_l, w_out_b, *, row_off, n_seq, seq_len, tm, tk):
    cl, sl = _seq_dft_matrices(seq_len)
    mt, kt = seq_len // tm, seq_len // tk
    ob, kb = row_off // tm, row_off // tk
    return pl.pallas_call(
        _fnet_seq_kernel,
        grid=(n_seq, mt, kt),
        in_specs=[
            pl.BlockSpec((tm, D), lambda s, i, k: (ob + s * mt + i, 0)),
            pl.BlockSpec((1, N_MOD, D), lambda s, i, k: ((row_off + s * seq_len) // T_CTX, 0, 0)),
            pl.BlockSpec((4, D), lambda s, i, k: (0, 0)),
            pl.BlockSpec((tm, tk), lambda s, i, k: (i, k)),
            pl.BlockSpec((tm, tk), lambda s, i, k: (i, k)),
            pl.BlockSpec((tk, D), lambda s, i, k: (kb + s * kt + k, 0)),
            pl.BlockSpec((tk, D), lambda s, i, k: (kb + s * kt + k, 0)),
            pl.BlockSpec((D, D), lambda s, i, k: (0, 0)),
        ],
        out_specs=pl.BlockSpec((tm, D), lambda s, i, k: (ob + s * mt + i, 0)),
        out_shape=jax.ShapeDtypeStruct((T, D), F32),
        scratch_shapes=[pltpu.VMEM((tm, D), F32)],
        input_output_aliases={0: 0},
        compiler_params=_cparams(("parallel", "parallel", "arbitrary")),
        name="fnet_seq_%d" % seq_len,
    )(x, mod_l, g_l, cl, sl, xc, xs, w_out_b)


def _fnet_mixer(x, mod_l, g_l, w_out):
    tm = 1024
    cc, sc = _dft_tables(FNET_GC, 1.0 / math.sqrt(FNET_GC))
    cs = jnp.asarray(np.concatenate([cc, sc], axis=1), BF16)
    xc, xs = pl.pallas_call(
        _fnet_chan_kernel,
        grid=(T // tm,),
        in_specs=[
            pl.BlockSpec((tm, D), lambda i: (i, 0)),
            pl.BlockSpec((1, N_MOD, D), lambda i: ((i * tm) // T_CTX, 0, 0)),
            pl.BlockSpec((4, D), lambda i: (0, 0)),
            pl.BlockSpec((FNET_GC, 2 * FNET_GC), lambda i: (0, 0)),
        ],
        out_specs=[pl.BlockSpec((tm, D), lambda i: (i, 0))] * 2,
        out_shape=[jax.ShapeDtypeStruct((T, D), BF16)] * 2,
        compiler_params=_cparams(("parallel",)),
        name="fnet_chan",
    )(x, mod_l, g_l, cs)
    w_out_b = w_out.astype(BF16)
    x = _fnet_seq(x, xc, xs, mod_l, g_l, w_out_b, row_off=0, n_seq=T_CTX // L_CTX, seq_len=L_CTX,
                  tm=L_CTX, tk=L_CTX)
    return _fnet_seq(x, xc, xs, mod_l, g_l, w_out_b, row_off=T_CTX, n_seq=N_LAT, seq_len=L_LAT,
                     tm=1024, tk=1024)


def _ssm_operators(lam_re, lam_im, b_re, b_im, c_re, c_im, log_dt):
    ops = []
    for d in range(2):
        lam = lax.complex(lam_re[d].astype(F32), lam_im[d].astype(F32))
        dt = jnp.exp(log_dt[d].astype(F32))[:, None]
        ldt = lam * dt
        lam_bar = jnp.exp(ldt)
        b_bar = ((lam_bar - 1.0) / lam)[..., None] * lax.complex(b_re[d].astype(F32), b_im[d].astype(F32))
        cmat = lax.complex(c_re[d].astype(F32), c_im[d].astype(F32))
        tau = jnp.arange(CHUNK + 1, dtype=F32)
        pw = jnp.exp(ldt[None] * tau[:, None, None])
        kern = jnp.einsum("ghp,tgp,gpk->tghk", cmat, pw[:CHUNK], b_bar).real
        ops.append((ldt, b_bar, cmat, pw, kern))

    t = np.arange(CHUNK)
    dif = t[:, None] - t[None, :]
    kf, kb = ops[0][4], ops[1][4]
    mf = jnp.where((dif >= 0)[:, :, None, None, None], kf[np.clip(dif, 0, CHUNK - 1)], 0.0)
    mb = jnp.where((dif <= 0)[:, :, None, None, None], kb[np.clip(-dif, 0, CHUNK - 1)], 0.0)
    mop = (mf + mb).transpose(2, 0, 3, 1, 4).reshape(SSM_G, CHUNK * SSM_GC, CHUNK * SSM_GC)

    pf, pb = ops[0][3], ops[1][3]
    bf = pf[CHUNK - 1 - t][:, :, :, None] * ops[0][1][None]
    bb = pb[t][:, :, :, None] * ops[1][1][None]
    def rows(z):
        return z.transpose(1, 2, 0, 3).reshape(SSM_G, SSM_P, CHUNK * SSM_GC)
    bst = jnp.concatenate([rows(bf.real), rows(bb.real), rows(bf.imag), rows(bb.imag)], axis=1)

    wf = ops[0][2][None] * pf[t + 1][:, :, None, :]
    wb = ops[1][2][None] * pb[CHUNK - t][:, :, None, :]
    def cols(z):
        return z.transpose(1, 0, 2, 3).reshape(SSM_G, CHUNK * SSM_GC, SSM_P)
    cst = jnp.concatenate([cols(wf.real), cols(wb.real), -cols(wf.imag), -cols(wb.imag)], axis=2)

    i = jnp.arange(CHUNK, dtype=F32)
    af = jnp.exp(ops[0][0][None] * (CHUNK * i)[:, None, None])
    ab = jnp.exp(ops[1][0][None] * (CHUNK * (CHUNK - 1 - i))[:, None, None])
    fix = jnp.concatenate([af, ab], axis=-1)
    a1 = jnp.concatenate([jnp.exp(ops[0][0] * CHUNK), jnp.exp(ops[1][0] * CHUNK)], axis=-1)
    a16 = jnp.concatenate([jnp.exp(ops[0][0] * CHUNK * CHUNK), jnp.exp(ops[1][0] * CHUNK * CHUNK)], axis=-1)

    def lanes(z):
        z = z.reshape(z.shape[0], SSM_G // GB, GB, 1, 2 * SSM_P)
        return jnp.moveaxis(z, 1, 0)
    tab = jnp.concatenate([lanes(a1[None].real), lanes(a1[None].imag), lanes(a16[None].real),
                           lanes(a16[None].imag)], axis=1)
    return (mop.astype(BF16), bst.astype(BF16), cst.astype(BF16), tab,
            lanes(fix.real), lanes(fix.imag))


N_STREAM = N_CHUNK // CHUNK
N_CTX_STREAM = T_CTX // L_CTX


def _ssm_kernel(at_ref, mop_ref, bst_ref, cst_ref, tab_ref, fre_ref, fim_ref, h0_ref,
                yt_ref, st_ref, ere, eim, cre, cim, bre, bim):
    lane = lax.broadcasted_iota(jnp.int32, (1, 1, 2 * SSM_P), 2)
    is_f = lane < SSM_P

    for k in range(GB):
        e = _dot(bst_ref[k], at_ref[k]).T
        ere[k] = e[:, :LANES]
        eim[k] = e[:, LANES:]

    a_re, a_im, a16_re, a16_im = tab_ref[0, 0], tab_ref[0, 1], tab_ref[0, 2], tab_ref[0, 3]

    c_re = jnp.zeros((GB, N_STREAM, 2 * SSM_P), F32)
    c_im = jnp.zeros((GB, N_STREAM, 2 * SSM_P), F32)
    for i in range(CHUNK):
        rf = pl.ds(i, N_STREAM, stride=CHUNK)
        rb = pl.ds(CHUNK - 1 - i, N_STREAM, stride=CHUNK)
        f_re, f_im, b_re, b_im = ere[:, rf, :], eim[:, rf, :], ere[:, rb, :], eim[:, rb, :]
        ere[:, rf, :] = jnp.where(is_f, c_re, f_re)
        eim[:, rf, :] = jnp.where(is_f, c_im, f_im)
        ere[:, rb, :] = jnp.where(is_f, b_re, c_re)
        eim[:, rb, :] = jnp.where(is_f, b_im, c_im)
        e_re = jnp.where(is_f, f_re, b_re)
        e_im = jnp.where(is_f, f_im, b_im)
        c_re, c_im = a_re * c_re - a_im * c_im + e_re, a_re * c_im + a_im * c_re + e_im
    cre[...] = c_re
    cim[...] = c_im
    st_ref[0, 0] = c_re[:, :N_CTX_STREAM, :]
    st_ref[0, 1] = c_im[:, :N_CTX_STREAM, :]

    s_re = h0_ref[0, 0]
    s_im = h0_ref[0, 1]
    for q in range(CHUNK):
        rf = pl.ds(N_CTX_STREAM + q, N_LAT, stride=CHUNK)
        rb = pl.ds(N_CTX_STREAM + CHUNK - 1 - q, N_LAT, stride=CHUNK)
        merge = q >= CHUNK // 2
        bre[:, rf, :] = jnp.where(is_f, s_re, bre[:, rf, :]) if merge else s_re
        bim[:, rf, :] = jnp.where(is_f, s_im, bim[:, rf, :]) if merge else s_im
        bre[:, rb, :] = jnp.where(is_f, bre[:, rb, :], s_re) if merge else s_re
        bim[:, rb, :] = jnp.where(is_f, bim[:, rb, :], s_im) if merge else s_im
        e_re = jnp.where(is_f, cre[:, rf, :], cre[:, rb, :])
        e_im = jnp.where(is_f, cim[:, rf, :], cim[:, rb, :])
        s_re, s_im = a16_re * s_re - a16_im * s_im + e_re, a16_re * s_im + a16_im * s_re + e_im

    lat0 = N_CTX_STREAM * CHUNK
    n_lat_stream = N_STREAM - N_CTX_STREAM
    b_re = bre[:, N_CTX_STREAM:, :]
    b_im = bim[:, N_CTX_STREAM:, :]
    for i in range(CHUNK):
        r = pl.ds(lat0 + i, n_lat_stream, stride=CHUNK)
        p_re = fre_ref[0, i]
        p_im = fim_ref[0, i]
        ere[:, r, :] += p_re * b_re - p_im * b_im
        eim[:, r, :] += p_re * b_im + p_im * b_re

    for k in range(GB):
        sin = jnp.concatenate([ere[k], eim[k]], axis=1)
        a = at_ref[k]
        yt = _dot(mop_ref[k], a) + _dot(cst_ref[k], sin.T.astype(BF16))
        yt_ref[k] = yt.astype(BF16)


def _ssm_pre_kernel(x_ref, m_ref, g_ref, o_ref):
    m = m_ref[0]
    g = g_ref[...]
    o_ref[...] = _pre(x_ref[...], g[0:1], m[1:2], m[0:1]).astype(BF16)


def _ssm_glu_kernel(x_ref, y_ref, m_ref, g_ref, d_ref, w_ref, o_ref):
    m = m_ref[0]
    g = g_ref[...]
    x = x_ref[...]
    hf = _pre(x, g[0:1], m[1:2], m[0:1])
    y = y_ref[...].astype(F32) + d_ref[...] * hf
    gl = jax.nn.gelu(y).astype(BF16)
    vg = _dot(gl, w_ref[...])
    out = vg[:, :D] * jax.nn.sigmoid(vg[:, D:])
    o_ref[...] = x + m[2:3] * _rms(out, g[1:2])


def _ssm_mixer(x, mod_l, g_l, state_in, lam_re, lam_im, b_re, b_im, c_re, c_im, log_dt, d_skip, w_glu):
    tm = 1024
    row_specs = [
        pl.BlockSpec((1, N_MOD, D), lambda i: ((i * tm) // T_CTX, 0, 0)),
        pl.BlockSpec((4, D), lambda i: (0, 0)),
    ]
    h = pl.pallas_call(
        _ssm_pre_kernel,
        grid=(T // tm,),
        in_specs=[pl.BlockSpec((tm, D), lambda i: (i, 0))] + row_specs,
        out_specs=pl.BlockSpec((tm, D), lambda i: (i, 0)),
        out_shape=jax.ShapeDtypeStruct((T, D), BF16),
        compiler_params=_cparams(("parallel",)),
        name="ssm_pre",
    )(x, mod_l, g_l)
    at = h.reshape(N_CHUNK, CHUNK, SSM_G, SSM_GC).transpose(2, 1, 3, 0).reshape(SSM_G, CHUNK * SSM_GC, N_CHUNK)

    mop, bst, cst, tab, fre, fim = _ssm_operators(lam_re, lam_im, b_re, b_im, c_re, c_im, log_dt)
    nb = SSM_G // GB
    sl = 2 * SSM_P
    h0 = state_in.astype(F32).transpose(4, 2, 0, 1, 3).reshape(2, nb, GB, N_LAT, sl).transpose(1, 0, 2, 3, 4)

    kd = CHUNK * SSM_GC
    gspec = pl.BlockSpec((GB, kd, kd), lambda b: (b, 0, 0))
    yt, st = pl.pallas_call(
        _ssm_kernel,
        grid=(nb,),
        in_specs=[
            pl.BlockSpec((GB, kd, N_CHUNK), lambda b: (b, 0, 0)),
            gspec, gspec, gspec,
            pl.BlockSpec((1, 4, GB, 1, sl), lambda b: (b, 0, 0, 0, 0)),
            pl.BlockSpec((1, CHUNK, GB, 1, sl), lambda b: (b, 0, 0, 0, 0)),
            pl.BlockSpec((1, CHUNK, GB, 1, sl), lambda b: (b, 0, 0, 0, 0)),
            pl.BlockSpec((1, 2, GB, N_LAT, sl), lambda b: (b, 0, 0, 0, 0)),
        ],
        out_specs=[
            pl.BlockSpec((GB, kd, N_CHUNK), lambda b: (b, 0, 0)),
            pl.BlockSpec((1, 2, GB, N_CTX_STREAM, sl), lambda b: (b, 0, 0, 0, 0)),
        ],
        out_shape=[
            jax.ShapeDtypeStruct((SSM_G, kd, N_CHUNK), BF16),
            jax.ShapeDtypeStruct((nb, 2, GB, N_CTX_STREAM, sl), F32),
        ],
        scratch_shapes=[
            pltpu.VMEM((GB, N_CHUNK, sl), F32), pltpu.VMEM((GB, N_CHUNK, sl), F32),
            pltpu.VMEM((GB, N_STREAM, sl), F32), pltpu.VMEM((GB, N_STREAM, sl), F32),
            pltpu.VMEM((GB, N_STREAM, sl), F32), pltpu.VMEM((GB, N_STREAM, sl), F32),
        ],
        compiler_params=_cparams(("parallel",)),
        name="ssm_scan",
    )(at, mop, bst, cst, tab, fre, fim, h0)

    y = yt.reshape(SSM_G, CHUNK, SSM_GC, N_CHUNK).transpose(3, 1, 0, 2).reshape(T, D)
    new_state = st.reshape(nb, 2, GB, N_CTX_STREAM, 2, SSM_P).transpose(3, 4, 0, 2, 5, 1)
    new_state = new_state.reshape(N_CTX_STREAM, 2, SSM_G, SSM_P, 2)

    x = pl.pallas_call(
        _ssm_glu_kernel,
        grid=(T // tm,),
        in_specs=[pl.BlockSpec((tm, D), lambda i: (i, 0)), pl.BlockSpec((tm, D), lambda i: (i, 0))] + row_specs + [
            pl.BlockSpec((1, D), lambda i: (0, 0)),
            pl.BlockSpec((D, 2 * D), lambda i: (0, 0)),
        ],
        out_specs=pl.BlockSpec((tm, D), lambda i: (i, 0)),
        out_shape=jax.ShapeDtypeStruct((T, D), F32),
        compiler_params=_cparams(("parallel",)),
        name="ssm_glu",
    )(x, y, mod_l, g_l, d_skip.reshape(1, D), w_glu.astype(BF16))
    return x, new_state


def _grid_pos_embed(n_tok, grid_w):
    rows = n_tok // grid_w
    r, col = jnp.meshgrid(jnp.arange(rows, dtype=F32), jnp.arange(grid_w, dtype=F32), indexing="ij")
    quarter = D // 4
    omega = 1.0 / (10000.0 ** (jnp.arange(quarter, dtype=F32) / quarter))
    ang_r = r.reshape(-1)[:, None] * omega
    ang_c = col.reshape(-1)[:, None] * omega
    return jnp.concatenate([jnp.sin(ang_r), jnp.cos(ang_r), jnp.sin(ang_c), jnp.cos(ang_c)], axis=-1)


def kernel(x_prompt, x_sample, state_ssm, c, c_ctx, ada_w, ada_b, norm_g, conv_w_in, conv_k, conv_w_out,
           fnet_w_out, ssm_lam_re, ssm_lam_im, ssm_b_re, ssm_b_im, ssm_c_re, ssm_c_im, ssm_log_dt, ssm_d,
           ssm_w_glu, ffn_w_gate, ffn_w_up, ffn_w_down, moe_router, moe_w_gate, moe_w_up, moe_w_down):
    depth = ada_w.shape[0]
    x_lat = x_sample + _grid_pos_embed(L_LAT, 64).astype(x_sample.dtype)
    x = jnp.concatenate([x_prompt.reshape(T_CTX, D), x_lat.reshape(N_LAT * L_LAT, D)], axis=0)
    mod = _modulation(jnp.concatenate([c_ctx[None], c], axis=0), ada_w, ada_b)

    states = []
    for l in range(depth):
        kind, j = l % 3, l // 3
        if kind == 0:
            x = _conv_mixer(x, mod[l], norm_g[l], conv_k[j], conv_w_in[j], conv_w_out[j])
        elif kind == 1:
            x = _fnet_mixer(x, mod[l], norm_g[l], fnet_w_out[j])
        else:
            x, s = _ssm_mixer(x, mod[l], norm_g[l], state_ssm[:, j], ssm_lam_re[j], ssm_lam_im[j],
                              ssm_b_re[j], ssm_b_im[j], ssm_c_re[j], ssm_c_im[j], ssm_log_dt[j], ssm_d[j],
                              ssm_w_glu[j])
            states.append(s)
        if l % 2 == 0:
            x = _ffn_dense(x, mod[l], norm_g[l], ffn_w_gate[l // 2], ffn_w_up[l // 2], ffn_w_down[l // 2])
        else:
            x = _ffn_moe(x, mod[l], norm_g[l], moe_router[l // 2], moe_w_gate[l // 2], moe_w_up[l // 2],
                         moe_w_down[l // 2])

    y_prompt = x[:T_CTX].reshape(x_prompt.shape)
    y_sample = x[T_CTX:].reshape(x_sample.shape)
    new_state = jnp.stack(states, axis=1).astype(x_prompt.dtype)
    return (y_prompt, y_sample, new_state)
```
